```python
import math
import jax, jax.numpy as jnp
from jax import lax
import numpy as np

D_MODEL = 1024
BATCH = 2
SEQ = 16384
DEPTH = 4

GRID_W = 64
CTX_LEN = 256

N_ATT_HEADS = 4
ATT_HEAD_DIM = 64
ATT_V_DIM = 2 * ATT_HEAD_DIM
QK_WIDTH = N_ATT_HEADS * 2 * ATT_HEAD_DIM
ATT_WIDTH = N_ATT_HEADS * ATT_V_DIM
CONV_WIDTH = D_MODEL // 4
CONV_WIDTH_K = 3
FOURIER_WIDTH = D_MODEL // 4
FOURIER_GROUPS = 4
FOURIER_GROUP_DIM = FOURIER_WIDTH // FOURIER_GROUPS

MIX_WIDTH = ATT_WIDTH + CONV_WIDTH + FOURIER_WIDTH
IN_WIDTH = 2 * QK_WIDTH + ATT_WIDTH + 3 * CONV_WIDTH + FOURIER_WIDTH
SPLITS = (QK_WIDTH, 2 * QK_WIDTH, 2 * QK_WIDTH + ATT_WIDTH,
          2 * QK_WIDTH + ATT_WIDTH + CONV_WIDTH,
          2 * QK_WIDTH + ATT_WIDTH + 2 * CONV_WIDTH,
          2 * QK_WIDTH + ATT_WIDTH + 3 * CONV_WIDTH)

D_FF = 4 * D_MODEL
Q_BLOCK = 128
ROPE_BASE = 10000.0
LN_EPS = 1e-5
RMS_EPS = 1e-5
DEEPNORM_ALPHA = (2 * DEPTH) ** 0.25
DEEPNORM_BETA = (8 * DEPTH) ** -0.25

kernel_name = "hymba_diffattn_conv_fnet_deepnorm_trunk"


def layer_norm(x, g, b):
    xf = x.astype(jnp.float32)
    mu = jnp.mean(xf, axis=-1, keepdims=True)
    var = jnp.mean(jnp.square(xf - mu), axis=-1, keepdims=True)
    y = (xf - mu) * lax.rsqrt(var + LN_EPS)
    return (y * g.astype(jnp.float32) + b.astype(jnp.float32)).astype(x.dtype)


def rms_norm(x, w):
    xf = x.astype(jnp.float32)
    y = xf * lax.rsqrt(jnp.mean(jnp.square(xf), axis=-1, keepdims=True) + RMS_EPS)
    return (y * w.astype(jnp.float32)).astype(x.dtype)


def modulate(x, shift, scale):
    return x * (1.0 + scale) + shift


def axial_rope(x, rows, cols):
    half = ATT_HEAD_DIM // 2
    quarter = half // 2
    inv_freq = 1.0 / (ROPE_BASE ** (jnp.arange(0, half, 2, dtype=jnp.float32) / half))

    def rot(xa, pos):
        ang = pos.astype(jnp.float32)[:, None] * inv_freq[None, :]
        cos = jnp.cos(ang)[None, :, None, None, :].astype(xa.dtype)
        sin = jnp.sin(ang)[None, :, None, None, :].astype(xa.dtype)
        x1, x2 = xa[..., :quarter], xa[..., quarter:]
        return jnp.concatenate([x1 * cos - x2 * sin, x2 * cos + x1 * sin], axis=-1)

    return jnp.concatenate([rot(x[..., :half], rows), rot(x[..., half:], cols)], axis=-1)


def split_proj(p):
    B, L, _ = p.shape
    q, k, v, u, gb, gc, f = jnp.split(p, SPLITS, axis=-1)
    q = q.reshape(B, L, N_ATT_HEADS, 2, ATT_HEAD_DIM)
    k = k.reshape(B, L, N_ATT_HEADS, 2, ATT_HEAD_DIM)
    v = v.reshape(B, L, N_ATT_HEADS, ATT_V_DIM)
    return q, k, v, u, gb, gc, f


def diff_attention(q, k, v, lam):
    s = jnp.einsum('bqhpd,bkhpd->bhpqk', q, k).astype(jnp.float32) * (ATT_HEAD_DIM ** -0.5)
    p = jax.nn.softmax(s, axis=-1)
    a = p[:, :, 0] - lam * p[:, :, 1]
    return jnp.einsum('bhqk,bkhe->bqhe', a.astype(v.dtype), v)


def diff_attention_blocked(q, k, v, lam):
    B, L = q.shape[0], q.shape[1]
    nb = L // Q_BLOCK
    qb = q.reshape(B, nb, Q_BLOCK, N_ATT_HEADS, 2, ATT_HEAD_DIM).transpose(1, 0, 2, 3, 4, 5)
    out = lax.map(lambda qi: diff_attention(qi, k, v, lam), qb)
    return out.transpose(1, 0, 2, 3, 4).reshape(B, L, N_ATT_HEADS, ATT_V_DIM)


def diff_head_out(o, subln_w, lam_init):
    B, L = o.shape[0], o.shape[1]
    return (rms_norm(o, subln_w) * (1.0 - lam_init)).reshape(B, L, ATT_WIDTH)


def short_conv(u, gb, gc, w, b):
    z = gc * u
    y = lax.conv_general_dilated(z, w[:, None, :].astype(z.dtype), window_strides=(1,),
                                 padding=((1, 1),), dimension_numbers=('NWC', 'WIO', 'NWC'),
                                 feature_group_count=CONV_WIDTH)
    return gb * (y + b)


def fourier_mix(f):
    B, L, _ = f.shape
    fg = f.reshape(B, L, FOURIER_GROUPS, FOURIER_GROUP_DIM).astype(jnp.float32)
    F = jnp.fft.fftn(fg, axes=(1, 3), norm='ortho')
    return jnp.real(F).reshape(B, L, FOURIER_WIDTH).astype(f.dtype)


def squared_relu_mlp(h, w_up, w_down):
    return jnp.square(jax.nn.relu(h @ w_up)) @ w_down


def setup_inputs(seed: int = 0) -> dict:
    key = jax.random.key(seed)
    ks = jax.random.split(key, 20)
    f32 = jnp.float32
    D = D_MODEL
    x = jax.random.normal(ks[0], (BATCH, SEQ, D), f32)
    c = jax.random.normal(ks[1], (BATCH, D), f32)
    ctx = jax.random.normal(ks[2], (BATCH, CTX_LEN, D), f32)
    c_ctx = jax.random.normal(ks[3], (D,), f32)
    w_mod = jax.random.normal(ks[4], (DEPTH, D, 6 * D), f32) * D ** -0.5
    b_mod = jax.random.normal(ks[5], (DEPTH, 6 * D), f32) * 0.02
    col_scale = jnp.ones((IN_WIDTH,), f32).at[2 * QK_WIDTH:2 * QK_WIDTH + ATT_WIDTH].set(DEEPNORM_BETA)
    w_in = jax.random.normal(ks[6], (DEPTH, D, IN_WIDTH), f32) * D ** -0.5 * col_scale
    diff_lambda = jax.random.normal(ks[7], (DEPTH, 4, ATT_HEAD_DIM), f32) * 0.1
    subln_w = 1.0 + 0.02 * jax.random.normal(ks[8], (DEPTH, ATT_V_DIM), f32)
    conv_w = jax.random.normal(ks[9], (DEPTH, CONV_WIDTH_K, CONV_WIDTH), f32) * CONV_WIDTH_K ** -0.5
    conv_b = jax.random.normal(ks[10], (DEPTH, CONV_WIDTH), f32) * 0.02
    w_out = jax.random.normal(ks[11], (DEPTH, MIX_WIDTH, D), f32) * MIX_WIDTH ** -0.5 * DEEPNORM_BETA
    ln1_g = 1.0 + 0.02 * jax.random.normal(ks[12], (DEPTH, D), f32)
    ln1_b = 0.02 * jax.random.normal(ks[13], (DEPTH, D), f32)
    w_up = jax.random.normal(ks[14], (DEPTH, D, D_FF), f32) * D ** -0.5 * DEEPNORM_BETA
    w_down = jax.random.normal(ks[15], (DEPTH, D_FF, D), f32) * D_FF ** -0.5 * DEEPNORM_BETA
    ln2_g = 1.0 + 0.02 * jax.random.normal(ks[16], (DEPTH, D), f32)
    ln2_b = 0.02 * jax.random.normal(ks[17], (DEPTH, D), f32)
    return {"x": x, "c": c, "ctx": ctx, "c_ctx": c_ctx, "w_mod": w_mod, "b_mod": b_mod,
            "w_in": w_in, "diff_lambda": diff_lambda, "subln_w": subln_w, "conv_w": conv_w,
            "conv_b": conv_b, "w_out": w_out, "ln1_g": ln1_g, "ln1_b": ln1_b, "w_up": w_up,
            "w_down": w_down, "ln2_g": ln2_g, "ln2_b": ln2_b}


def reference(x, c, ctx, c_ctx, w_mod, b_mod, w_in, diff_lambda, subln_w, conv_w, conv_b,
              w_out, ln1_g, ln1_b, w_up, w_down, ln2_g, ln2_b):
    B, S, D = x.shape
    ROWS = S // GRID_W
    rows = jnp.repeat(jnp.arange(ROWS), GRID_W)
    cols = jnp.tile(jnp.arange(GRID_W), ROWS)
    silu_c = jax.nn.silu(c)
    silu_cc = jax.nn.silu(c_ctx)

    for l in range(DEPTH):
        last = l == DEPTH - 1
        mod_x = (silu_c @ w_mod[l] + b_mod[l])[:, None, :]
        mod_c = (silu_cc @ w_mod[l] + b_mod[l])[None, None, :]
        sh1, sc1, g1, sh2, sc2, g2 = jnp.split(mod_x, 6, axis=-1)
        csh1, csc1, cg1, csh2, csc2, cg2 = jnp.split(mod_c, 6, axis=-1)

        lam_init = 0.8 - 0.6 * math.exp(-0.3 * l)
        lam_p = diff_lambda[l].astype(jnp.float32)
        lam = (jnp.exp(jnp.sum(lam_p[0] * lam_p[1])) - jnp.exp(jnp.sum(lam_p[2] * lam_p[3]))
               + lam_init)

        px = modulate(x, sh1, sc1) @ w_in[l]
        pc = modulate(ctx, csh1, csc1) @ w_in[l]
        qx, kx, vx, ux, gbx, gcx, fx = split_proj(px)
        qc, kc, vc, uc, gbc, gcc, fc = split_proj(pc)
        qx = axial_rope(qx, rows, cols)
        kx = axial_rope(kx, rows, cols)
        k_all = jnp.concatenate([kc, kx], axis=1)
        v_all = jnp.concatenate([vc, vx], axis=1)

        ox = diff_attention_blocked(qx, k_all, v_all, lam)
        mix_x = jnp.concatenate([
            diff_head_out(ox, subln_w[l], lam_init),
            short_conv(ux, gbx, gcx, conv_w[l], conv_b[l]),
            fourier_mix(fx)], axis=-1) @ w_out[l]
        x = layer_norm(DEEPNORM_ALPHA * x + g1 * mix_x, ln1_g[l], ln1_b[l])

        x = layer_norm(DEEPNORM_ALPHA * x + g2 * squared_relu_mlp(modulate(x, sh2, sc2), w_up[l], w_down[l]),
                       ln2_g[l], ln2_b[l])

        if not last:
            oc = diff_attention(qc, kc, vc, lam)
            mix_c = jnp.concatenate([
                diff_head_out(oc, subln_w[l], lam_init),
                short_conv(uc, gbc, gcc, conv_w[l], conv_b[l]),
                fourier_mix(fc)], axis=-1) @ w_out[l]
            ctx = layer_norm(DEEPNORM_ALPHA * ctx + cg1 * mix_c, ln1_g[l], ln1_b[l])
            ctx = layer_norm(DEEPNORM_ALPHA * ctx + cg2 * squared_relu_mlp(modulate(ctx, csh2, csc2), w_up[l], w_down[l]),
                             ln2_g[l], ln2_b[l])
    return x
```

```python
import functools
import math

import numpy as np
import jax
import jax.numpy as jnp
from jax import lax
from jax.experimental import pallas as pl
from jax.experimental.pallas import tpu as pltpu

F32 = jnp.float32
BF16 = jnp.bfloat16

GRID_W = 64
N_HEADS = 4
HEAD_DIM = 64
V_DIM = 2 * HEAD_DIM
QK_W = N_HEADS * 2 * HEAD_DIM
ATT_W = N_HEADS * V_DIM
CONV_W = 256
FOUR_W = 256
FOUR_GDIM = 64
ROPE_BASE = 10000.0
LN_EPS = 1e-5
RMS_EPS = 1e-5
ROPE_HALF = HEAD_DIM // 2
ROPE_QUARTER = ROPE_HALF // 2

VMEM_LIMIT = 56 * 1024 * 1024


def _cparams(sem):
    return pltpu.CompilerParams(dimension_semantics=sem, vmem_limit_bytes=VMEM_LIMIT)


def _layer_norm(h, g, b):
    mu = jnp.mean(h, axis=-1, keepdims=True)
    d = h - mu
    var = jnp.mean(d * d, axis=-1, keepdims=True)
    return d * lax.rsqrt(var + LN_EPS) * g + b


def _mod_kernel(cs_ref, w_ref, b_ref, o_ref):
    cs = cs_ref[...]
    s = cs * jax.nn.sigmoid(cs)
    o_ref[0] = jnp.dot(s.astype(BF16), w_ref[0].astype(BF16), preferred_element_type=F32) + b_ref[0]


def _modulation(cs, w_mod, b_mod):
    depth, d, n = w_mod.shape
    tn = n // 4
    return pl.pallas_call(
        _mod_kernel,
        grid=(depth, n // tn),
        in_specs=[pl.BlockSpec((8, d), lambda l, j: (0, 0)),
                  pl.BlockSpec((1, d, tn), lambda l, j: (l, 0, j)),
                  pl.BlockSpec((1, 1, tn), lambda l, j: (l, 0, j))],
        out_specs=pl.BlockSpec((1, 8, tn), lambda l, j: (l, 0, j)),
        out_shape=jax.ShapeDtypeStruct((depth, 8, n), F32),
        compiler_params=_cparams(("arbitrary", "arbitrary")),
        name="modulation",
    )(cs, w_mod, b_mod.reshape(depth, 1, n))


def _inproj_kernel(x_ref, mod_ref, wq_ref, wkt_ref, wv_ref, wug_ref, wf_ref, e_ref,
                   cq_ref, sqa_ref, sqb_ref, ck_ref, sk_ref,
                   q_ref, kt_ref, v_ref, z_ref, gb_ref, zf_ref):
    x = x_ref[0]
    sh = mod_ref[0, 0:1, :]
    sc = mod_ref[0, 1:2, :]
    xm = (x * (1.0 + sc) + sh).astype(BF16)

    q = jnp.dot(xm, wq_ref[...], preferred_element_type=F32)
    reps = QK_W // cq_ref.shape[1]
    cq = jnp.tile(cq_ref[...], (1, reps))
    sqa = jnp.tile(sqa_ref[...], (1, reps))
    sqb = jnp.tile(sqb_ref[...], (1, reps))
    q = (q * cq + pltpu.roll(q, QK_W - ROPE_QUARTER, 1) * sqa + pltpu.roll(q, ROPE_QUARTER, 1) * sqb)
    q_ref[0] = q.astype(BF16)

    kt = lax.dot_general(wkt_ref[...], xm, (((1,), (1,)), ((), ())), preferred_element_type=F32)
    pieces = []
    for g in range(QK_W // ROPE_HALF):
        lo = g * ROPE_HALF
        pieces.append(kt[lo + ROPE_QUARTER:lo + ROPE_HALF])
        pieces.append(kt[lo:lo + ROPE_QUARTER])
    ksw = jnp.concatenate(pieces, axis=0)
    kreps = QK_W // ck_ref.shape[0]
    ck = jnp.tile(ck_ref[...], (kreps, 1))
    sk = jnp.tile(sk_ref[...], (kreps, 1))
    kt_ref[0] = (kt * ck + ksw * sk).astype(BF16)

    v_ref[0] = jnp.dot(xm, wv_ref[...], preferred_element_type=F32).astype(BF16)

    ug = jnp.dot(xm, wug_ref[...], preferred_element_type=F32)
    z_ref[0] = ug[:, 2 * CONV_W:3 * CONV_W] * ug[:, 0:CONV_W]
    gb_ref[0] = ug[:, CONV_W:2 * CONV_W]

    f = jnp.dot(xm, wf_ref[...], preferred_element_type=F32)
    zz = jnp.dot(f.astype(BF16), e_ref[...], preferred_element_type=F32)
    zf_ref[0, 0] = zz[:, :FOUR_W].astype(BF16)
    zf_ref[0, 1] = zz[:, FOUR_W:].astype(BF16)


def _inproj(x, mod, w, e_mat, tabs, tm):
    b, l, d = x.shape
    nt = l // tm
    cq, sqa, sqb, ck, sk = tabs
    const = lambda shape: pl.BlockSpec(shape, lambda i, bb: (0,) * len(shape))
    in_specs = [
        pl.BlockSpec((1, tm, d), lambda i, bb: (bb, i, 0)),
        pl.BlockSpec((1, 6, d), lambda i, bb: (bb, 0, 0)),
        const((d, QK_W)), const((QK_W, d)), const((d, ATT_W)), const((d, 3 * CONV_W)),
        const((d, FOUR_W)), const((FOUR_W, 2 * FOUR_W)),
        pl.BlockSpec((tm, 2 * HEAD_DIM), lambda i, bb: (i, 0)),
        pl.BlockSpec((tm, 2 * HEAD_DIM), lambda i, bb: (i, 0)),
        pl.BlockSpec((tm, 2 * HEAD_DIM), lambda i, bb: (i, 0)),
        pl.BlockSpec((2 * HEAD_DIM, tm), lambda i, bb: (0, i)),
        pl.BlockSpec((2 * HEAD_DIM, tm), lambda i, bb: (0, i)),
    ]
    out_specs = [
        pl.BlockSpec((1, tm, QK_W), lambda i, bb: (bb, i, 0)),
        pl.BlockSpec((1, QK_W, tm), lambda i, bb: (bb, 0, i)),
        pl.BlockSpec((1, tm, ATT_W), lambda i, bb: (bb, i, 0)),
        pl.BlockSpec((1, tm, CONV_W), lambda i, bb: (bb, i, 0)),
        pl.BlockSpec((1, tm, CONV_W), lambda i, bb: (bb, i, 0)),
        pl.BlockSpec((1, 2, tm, FOUR_W), lambda i, bb: (bb, 0, i, 0)),
    ]
    out_shape = [
        jax.ShapeDtypeStruct((b, l, QK_W), BF16),
        jax.ShapeDtypeStruct((b, QK_W, l), BF16),
        jax.ShapeDtypeStruct((b, l, ATT_W), BF16),
        jax.ShapeDtypeStruct((b, l, CONV_W), F32),
        jax.ShapeDtypeStruct((b, l, CONV_W), F32),
        jax.ShapeDtypeStruct((b, 2, l, FOUR_W), BF16),
    ]
    return pl.pallas_call(
        _inproj_kernel, grid=(nt, b), in_specs=in_specs, out_specs=out_specs, out_shape=out_shape,
        compiler_params=_cparams(("arbitrary", "arbitrary")), name="inproj",
    )(x, mod, w["wq"], w["wkt"], w["wv"], w["wug"], w["wf"], e_mat, cq, sqa, sqb, ck, sk)


def _attn_kernel(*refs, lam_init, tk, n_chunks, has_ctx):
    if has_ctx:
        lam_ref, sw_ref, q_ref, kx_ref, vx_ref, kc_ref, vc_ref, o_ref = refs
    else:
        lam_ref, sw_ref, q_ref, kx_ref, vx_ref, o_ref = refs
    q = q_ref[0]
    tq = q.shape[0]
    lane = lax.broadcasted_iota(jnp.int32, q.shape, 1)
    zero = jnp.zeros_like(q)
    qs = jnp.concatenate([jnp.where(lane < HEAD_DIM, q, zero), jnp.where(lane >= HEAD_DIM, q, zero)], axis=0)

    def update(carry, kc, vc):
        m, l, acc = carry
        s = jnp.dot(qs, kc, preferred_element_type=F32)
        mn = jnp.maximum(m, jnp.max(s, axis=1, keepdims=True))
        alpha = jnp.exp(m - mn)
        p = jnp.exp(s - mn)
        l = alpha * l + jnp.sum(p, axis=1, keepdims=True)
        acc = alpha * acc + jnp.dot(p.astype(BF16), vc, preferred_element_type=F32)
        return mn, l, acc

    carry = (jnp.full((2 * tq, 1), -1e30, F32), jnp.zeros((2 * tq, 1), F32), jnp.zeros((2 * tq, V_DIM), F32))
    if has_ctx:
        carry = update(carry, kc_ref[0], vc_ref[0])

    def body(j, c):
        off = pl.multiple_of(j * tk, tk)
        return update(c, kx_ref[0, :, pl.ds(off, tk)], vx_ref[0, pl.ds(off, tk), :])

    _, l, acc = lax.fori_loop(0, n_chunks, body, carry)
    o = acc / l

    lp = lam_ref[...]
    lam = (jnp.exp(jnp.sum(lp[0:1] * lp[1:2], axis=1, keepdims=True))
           - jnp.exp(jnp.sum(lp[2:3] * lp[3:4], axis=1, keepdims=True)) + lam_init)
    d = o[:tq] - lam * o[tq:]
    ms = jnp.mean(d * d, axis=1, keepdims=True)
    y = d * lax.rsqrt(ms + RMS_EPS) * sw_ref[...] * (1.0 - lam_init)
    o_ref[0] = y.astype(BF16)


def _attention(lam_p, subw, q, kx, vx, kc, vc, lam_init, tq, tk):
    b, l, _ = q.shape
    lk = kx.shape[2]
    has_ctx = kc is not None
    grid = (b, N_HEADS, l // tq)
    in_specs = [
        pl.BlockSpec((4, HEAD_DIM), lambda bb, h, i: (0, 0)),
        pl.BlockSpec((1, V_DIM), lambda bb, h, i: (0, 0)),
        pl.BlockSpec((1, tq, 2 * HEAD_DIM), lambda bb, h, i: (bb, i, h)),
        pl.BlockSpec((1, 2 * HEAD_DIM, lk), lambda bb, h, i: (bb, h, 0)),
        pl.BlockSpec((1, lk, V_DIM), lambda bb, h, i: (bb, 0, h)),
    ]
    args = [lam_p, subw, q, kx, vx]
    if has_ctx:
        lc = kc.shape[2]
        in_specs += [pl.BlockSpec((1, 2 * HEAD_DIM, lc), lambda bb, h, i: (bb, h, 0)),
                     pl.BlockSpec((1, lc, V_DIM), lambda bb, h, i: (bb, 0, h))]
        args += [kc, vc]
    return pl.pallas_call(
        functools.partial(_attn_kernel, lam_init=lam_init, tk=tk, n_chunks=lk // tk, has_ctx=has_ctx),
        grid=grid, in_specs=in_specs,
        out_specs=pl.BlockSpec((1, tq, V_DIM), lambda bb, h, i: (bb, i, h)),
        out_shape=jax.ShapeDtypeStruct((b, l, ATT_W), BF16),
        compiler_params=_cparams(("arbitrary", "arbitrary", "arbitrary")), name="diff_attention",
    )(*args)


def _fa_kernel(z_ref, wa_ref, tc_ref, ts_ref, y_ref):
    y = jnp.dot(wa_ref[...], z_ref[0], preferred_element_type=F32)
    n1 = y.shape[0] // 2
    yr, yi = y[:n1], y[n1:]
    tc, ts = tc_ref[...], ts_ref[...]
    y_ref[0, :n1] = (yr * tc + yi * ts).astype(BF16)
    y_ref[0, n1:] = (yi * tc - yr * ts).astype(BF16)


def _fc_kernel(y_ref, wc_ref, o_ref, *, kb):
    for kk in range(kb):
        ri = jnp.concatenate([y_ref[0, 0, kk], y_ref[0, 1, kk]], axis=0)
        o_ref[0, kk] = jnp.dot(wc_ref[...], ri, preferred_element_type=F32).astype(BF16)


def _seq_dft(zf, n1, n2, wa, wc, tc, ts):
    b = zf.shape[0]
    c = zf.shape[3]
    cols = n2 * c
    cb = min(cols, 2048)
    y = pl.pallas_call(
        _fa_kernel, grid=(cols // cb, b),
        in_specs=[pl.BlockSpec((1, 2 * n1, cb), lambda j, bb: (bb, 0, j)),
                  pl.BlockSpec((2 * n1, 2 * n1), lambda j, bb: (0, 0)),
                  pl.BlockSpec((n1, cb), lambda j, bb: (0, j)),
                  pl.BlockSpec((n1, cb), lambda j, bb: (0, j))],
        out_specs=pl.BlockSpec((1, 2 * n1, cb), lambda j, bb: (bb, 0, j)),
        out_shape=jax.ShapeDtypeStruct((b, 2 * n1, cols), BF16),
        compiler_params=_cparams(("arbitrary", "arbitrary")), name="seq_dft_a",
    )(zf.reshape(b, 2 * n1, cols), wa, tc, ts)
    kb = 8
    return pl.pallas_call(
        functools.partial(_fc_kernel, kb=kb), grid=(b, n1 // kb),
        in_specs=[pl.BlockSpec((1, 2, kb, n2, c), lambda bb, j: (bb, 0, j, 0, 0)),
                  pl.BlockSpec((n2, 2 * n2), lambda bb, j: (0, 0))],
        out_specs=pl.BlockSpec((1, kb, n2, c), lambda bb, j: (bb, j, 0, 0)),
        out_shape=jax.ShapeDtypeStruct((b, n1, n2, c), BF16),
        compiler_params=_cparams(("arbitrary", "arbitrary")), name="seq_dft_c",
    )(y.reshape(b, 2, n1, n2, c), wc)


def _dft_direct_kernel(z_ref, w_ref, o_ref):
    ri = jnp.concatenate([z_ref[0, 0], z_ref[0, 1]], axis=0)
    o_ref[0] = jnp.dot(w_ref[...], ri, preferred_element_type=F32).astype(BF16)


def _seq_dft_direct(zf, w):
    b, _, l, c = zf.shape
    return pl.pallas_call(
        _dft_direct_kernel, grid=(b,),
        in_specs=[pl.BlockSpec((1, 2, l, c), lambda bb: (bb, 0, 0, 0)),
                  pl.BlockSpec((l, 2 * l), lambda bb: (0, 0))],
        out_specs=pl.BlockSpec((1, l, c), lambda bb: (bb, 0, 0)),
        out_shape=jax.ShapeDtypeStruct((b, l, c), BF16),
        compiler_params=_cparams(("arbitrary",)), name="seq_dft_direct",
    )(zf, w)


def _outproj_kernel(o_ref, z_ref, zp_ref, zn_ref, gb_ref, fo_ref, x_ref, mod_ref, wo_ref,
                    cw_ref, cb_ref, g_ref, b_ref, out_ref, *, alpha, fo_pieces):
    i = pl.program_id(1)
    nt = pl.num_programs(1)
    z = z_ref[0]
    tm = z.shape[0]
    row = lax.broadcasted_iota(jnp.int32, z.shape, 0)
    prev_row = zp_ref[0, 7:8, :] * (i > 0).astype(F32)
    next_row = zn_ref[0, 0:1, :] * (i < nt - 1).astype(F32)
    z_prev = jnp.where(row == 0, prev_row, pltpu.roll(z, 1, 0))
    z_next = jnp.where(row == tm - 1, next_row, pltpu.roll(z, tm - 1, 0))
    cw = cw_ref[...]
    y = gb_ref[0] * (cw[0:1] * z_prev + cw[1:2] * z + cw[2:3] * z_next + cb_ref[...])

    fo = fo_ref[0]
    if fo_pieces > 1:
        fo = jnp.concatenate([fo[:, j * FOUR_W:(j + 1) * FOUR_W] for j in range(fo_pieces)], axis=0)
    mix = jnp.dot(o_ref[0], wo_ref[0:ATT_W], preferred_element_type=F32)
    mix += jnp.dot(y.astype(BF16), wo_ref[ATT_W:ATT_W + CONV_W], preferred_element_type=F32)
    mix += jnp.dot(fo, wo_ref[ATT_W + CONV_W:], preferred_element_type=F32)
    h = alpha * x_ref[0] + mod_ref[0, 2:3, :] * mix
    out_ref[0] = _layer_norm(h, g_ref[...], b_ref[...])


def _outproj(o, z, gb, fo, x, mod, wo, cw, cb, g, bln, alpha, tm, n1):
    b, l, d = x.shape
    nt = l // tm
    r8 = tm // 8
    nb8 = l // 8
    if fo.ndim == 4:
        pieces = tm // n1
        fo = fo.reshape(b, n1, -1)
        fo_spec = pl.BlockSpec((1, n1, pieces * FOUR_W), lambda bb, i: (bb, 0, i))
    else:
        pieces = 1
        fo_spec = pl.BlockSpec((1, tm, FOUR_W), lambda bb, i: (bb, i, 0))
    row = lambda w: pl.BlockSpec((1, w), lambda bb, i: (0, 0))
    in_specs = [
        pl.BlockSpec((1, tm, ATT_W), lambda bb, i: (bb, i, 0)),
        pl.BlockSpec((1, tm, CONV_W), lambda bb, i: (bb, i, 0)),
        pl.BlockSpec((1, 8, CONV_W), lambda bb, i: (bb, jnp.maximum(i * r8 - 1, 0), 0)),
        pl.BlockSpec((1, 8, CONV_W), lambda bb, i: (bb, jnp.minimum((i + 1) * r8, nb8 - 1), 0)),
        pl.BlockSpec((1, tm, CONV_W), lambda bb, i: (bb, i, 0)),
        fo_spec,
        pl.BlockSpec((1, tm, d), lambda bb, i: (bb, i, 0)),
        pl.BlockSpec((1, 6, d), lambda bb, i: (bb, 0, 0)),
        pl.BlockSpec((d, d), lambda bb, i: (0, 0)),
        pl.BlockSpec((3, CONV_W), lambda bb, i: (0, 0)),
        row(CONV_W), row(d), row(d),
    ]
    return pl.pallas_call(
        functools.partial(_outproj_kernel, alpha=alpha, fo_pieces=pieces),
        grid=(b, nt), in_specs=in_specs,
        out_specs=pl.BlockSpec((1, tm, d), lambda bb, i: (bb, i, 0)),
        out_shape=jax.ShapeDtypeStruct((b, l, d), F32),
        compiler_params=_cparams(("arbitrary", "arbitrary")), name="outproj_ln",
    )(o, z, z, z, gb, fo, x, mod, wo, cw, cb, g, bln)


def _mlp_kernel(x_ref, mod_ref, wu_ref, wd_ref, g_ref, b_ref, out_ref, *, alpha, fc):
    x = x_ref[0]
    sh = mod_ref[0, 3:4, :]
    sc = mod_ref[0, 4:5, :]
    xm = (x * (1.0 + sc) + sh).astype(BF16)
    acc = jnp.zeros(x.shape, F32)
    for c in range(wu_ref.shape[1] // fc):
        h = jnp.dot(xm, wu_ref[:, c * fc:(c + 1) * fc], preferred_element_type=F32)
        h = jnp.square(jnp.maximum(h, 0.0)).astype(BF16)
        acc += jnp.dot(h, wd_ref[c * fc:(c + 1) * fc, :], preferred_element_type=F32)
    h = alpha * x + mod_ref[0, 5:6, :] * acc
    out_ref[0] = _layer_norm(h, g_ref[...], b_ref[...])


def _mlp(x, mod, wu, wd, g, bln, alpha, tm):
    b, l, d = x.shape
    dff = wu.shape[1]
    row = lambda w: pl.BlockSpec((1, w), lambda bb, i: (0, 0))
    return pl.pallas_call(
        functools.partial(_mlp_kernel, alpha=alpha, fc=1024),
        grid=(b, l // tm),
        in_specs=[pl.BlockSpec((1, tm, d), lambda bb, i: (bb, i, 0)),
                  pl.BlockSpec((1, 6, d), lambda bb, i: (bb, 0, 0)),
                  pl.BlockSpec((d, dff), lambda bb, i: (0, 0)),
                  pl.BlockSpec((dff, d), lambda bb, i: (0, 0)),
                  row(d), row(d)],
        out_specs=pl.BlockSpec((1, tm, d), lambda bb, i: (bb, i, 0)),
        out_shape=jax.ShapeDtypeStruct((b, l, d), F32),
        compiler_params=_cparams(("arbitrary", "arbitrary")), name="mlp_ln",
    )(x, mod, wu, wd, g, bln)


def _rope_tables(s, scale):
    t = jnp.arange(s)
    rows = (t // GRID_W).astype(F32)
    cols = (t % GRID_W).astype(F32)
    inv_freq = 1.0 / (ROPE_BASE ** (jnp.arange(0, ROPE_HALF, 2, dtype=F32) / ROPE_HALF))
    dd = np.arange(HEAD_DIM)
    first = jnp.asarray((dd % ROPE_HALF) < ROPE_QUARTER)[None, :]
    use_col = jnp.asarray(dd >= ROPE_HALF)[None, :]
    pos = jnp.where(use_col, cols[:, None], rows[:, None])
    ang = pos * inv_freq[dd % ROPE_QUARTER][None, :]
    cos = jnp.cos(ang)
    sg = jnp.sin(ang) * jnp.where(first, -1.0, 1.0)
    two = lambda a: jnp.tile(a, (1, 2))
    cq = two(cos) * scale
    sqa = two(jnp.where(first, sg, 0.0)) * scale
    sqb = two(jnp.where(first, 0.0, sg)) * scale
    return cq, sqa, sqb, two(cos).T, two(sg).T


def _plain_tables(s, scale):
    w = 2 * HEAD_DIM
    zq = jnp.zeros((s, w), F32)
    return jnp.full((s, w), scale, F32), zq, zq, jnp.ones((w, s), F32), jnp.zeros((w, s), F32)


def _dft_cos_sin(n):
    k = np.arange(n)
    ang = 2.0 * np.pi * ((k[:, None] * k[None, :]) % n) / n
    return np.cos(ang), np.sin(ang)


def _channel_dft_matrix():
    c, s = _dft_cos_sin(FOUR_GDIM)
    groups = FOUR_W // FOUR_GDIM
    eye = np.eye(groups)
    return jnp.asarray(np.concatenate([np.kron(eye, c), -np.kron(eye, s)], axis=1), BF16)


def _seq_dft_consts(n1, n2, c):
    s = n1 * n2
    c1, s1 = _dft_cos_sin(n1)
    wa = jnp.asarray(np.block([[c1, s1], [-s1, c1]]), BF16)
    c2, s2 = _dft_cos_sin(n2)
    norm = 1.0 / math.sqrt(s * FOUR_GDIM)
    wc = jnp.asarray(np.concatenate([c2, s2], axis=1) * norm, BF16)
    k1 = jnp.arange(n1)[:, None]
    t2 = jnp.arange(n2)[None, :]
    ang = (2.0 * math.pi / s) * ((k1 * t2) % s).astype(F32)
    expand = lambda a: jnp.broadcast_to(a[:, :, None], (n1, n2, c)).reshape(n1, n2 * c)
    return wa, wc, expand(jnp.cos(ang)), expand(jnp.sin(ang))


def _direct_dft_matrix(l):
    c, s = _dft_cos_sin(l)
    return jnp.asarray(np.concatenate([c, s], axis=1) / math.sqrt(l * FOUR_GDIM), BF16)


def kernel(x, c, ctx, c_ctx, w_mod, b_mod, w_in, diff_lambda, subln_w, conv_w, conv_b, w_out,
           ln1_g, ln1_b, w_up, w_down, ln2_g, ln2_b):
    b, s, d = x.shape
    lc = ctx.shape[1]
    depth = w_in.shape[0]
    alpha = (2 * depth) ** 0.25
    n1 = math.isqrt(s)
    n2 = s // n1
    assert n1 * n2 == s and s % GRID_W == 0 and b + 1 <= 8

    tm_in = min(512, s)
    tm = min(1024, s)
    tm_mlp = min(512, s)
    tq = min(512, s)
    tk = min(512, s)

    cs = jnp.zeros((8, d), F32).at[:b].set(c).at[b].set(c_ctx)
    mod = _modulation(cs, w_mod, b_mod)

    w_in_b = w_in.astype(BF16)
    o_k, o_v, o_u, o_f = QK_W, 2 * QK_W, 2 * QK_W + ATT_W, 2 * QK_W + ATT_W + 3 * CONV_W
    wq_all = w_in_b[:, :, :o_k]
    wkt_all = jnp.swapaxes(w_in_b[:, :, o_k:o_v], 1, 2)
    wv_all = w_in_b[:, :, o_v:o_u]
    wug_all = w_in_b[:, :, o_u:o_f]
    wf_all = w_in_b[:, :, o_f:]
    wo_all = w_out.astype(BF16)
    wu_all = w_up.astype(BF16)
    wd_all = w_down.astype(BF16)

    scale = HEAD_DIM ** -0.5
    tabs_x = _rope_tables(s, scale)
    tabs_c = _plain_tables(lc, scale)
    e_mat = _channel_dft_matrix()
    wa, wc, tc, ts = _seq_dft_consts(n1, n2, FOUR_W)
    w_dft_c = _direct_dft_matrix(lc)

    for l in range(depth):
        last = l == depth - 1
        lam_init = 0.8 - 0.6 * math.exp(-0.3 * l)
        mod_x = mod[l, :b].reshape(b, 6, d)
        mod_c = jnp.broadcast_to(mod[l, b].reshape(1, 6, d), (b, 6, d))
        w = {"wq": wq_all[l], "wkt": wkt_all[l], "wv": wv_all[l], "wug": wug_all[l], "wf": wf_all[l]}
        lam_p = diff_lambda[l].astype(F32)
        subw = subln_w[l].reshape(1, V_DIM)
        cw = conv_w[l]
        cb = conv_b[l].reshape(1, CONV_W)
        g1, b1 = ln1_g[l].reshape(1, d), ln1_b[l].reshape(1, d)
        g2, b2 = ln2_g[l].reshape(1, d), ln2_b[l].reshape(1, d)

        qx, ktx, vx, zx, gbx, zfx = _inproj(x, mod_x, w, e_mat, tabs_x, tm_in)
        qc, ktc, vc, zc, gbc, zfc = _inproj(ctx, mod_c, w, e_mat, tabs_c, lc)

        ox = _attention(lam_p, subw, qx, ktx, vx, ktc, vc, lam_init, tq, tk)
        fox = _seq_dft(zfx, n1, n2, wa, wc, tc, ts)
        x = _outproj(ox, zx, gbx, fox, x, mod_x, wo_all[l], cw, cb, g1, b1, alpha, tm, n1)
        x = _mlp(x, mod_x, wu_all[l], wd_all[l], g2, b2, alpha, tm_mlp)

        if not last:
            oc = _attention(lam_p, subw, qc, ktc, vc, None, None, lam_init, lc, lc)
            foc = _seq_dft_direct(zfc, w_dft_c)
            ctx = _outproj(oc, zc, gbc, foc, ctx, mod_c, wo_all[l], cw, cb, g1, b1, alpha, lc, n1)
            ctx = _mlp(ctx, mod_c, wu_all[l], wd_all[l], g2, b2, alpha, lc)
    return x
```
